```python
import math
import jax, jax.numpy as jnp
from jax import lax
import numpy as np

D_MODEL = 1024
BATCH = 2
SEQ = 8192
DEPTH = 4
DEC_BATCH = 16
DEC_SEQ = 2048
PAST_LEN = 128

N_HEADS = 8
N_KV_HEADS = 2
HEAD_DIM = 128
Q_BLOCK = 128
ROPE_THETA = 10000.0
GRID_W = 64
D_HYENA = 512
SHORT_CONV = 3
POS_BANDS = 16
POS_DIM = 1 + 2 * POS_BANDS
FILTER_HIDDEN = 64
DECAY_FAST_PCT = 0.3
DECAY_SLOW_PCT = 1.5
DECAY_TARGET = 1e-2
N_MEM = 256
N_XHEADS = 4
XHEAD_DIM = D_MODEL // N_XHEADS
N_EXPERTS = 16
EC_CAPACITY = 2
D_EXPERT = 2 * D_MODEL
EPS = 1e-6
Q_W = N_HEADS * HEAD_DIM
KV_W = N_KV_HEADS * HEAD_DIM
HY_W = 3 * D_HYENA
GATE_W = 2 * D_MODEL
D_IN = Q_W + 2 * KV_W + HY_W + GATE_W

kernel_name = 'hybrid_attn_hyena_ec_moe_encoder'

F32 = jnp.float32


def rms_norm(x, g):
    xf = x.astype(F32)
    y = xf * lax.rsqrt(jnp.mean(xf * xf, axis=-1, keepdims=True) + EPS)
    return (y * g.astype(F32)).astype(x.dtype)


def axial_rope_tables(L):
    rows_n = L // GRID_W
    row = jnp.repeat(jnp.arange(rows_n, dtype=F32), GRID_W)
    col = jnp.tile(jnp.arange(GRID_W, dtype=F32), rows_n)
    half = HEAD_DIM // 2
    inv = ROPE_THETA ** (-jnp.arange(0, half, 2, dtype=F32) / half)
    ang_r = row[:, None] * inv[None, :]
    ang_c = col[:, None] * inv[None, :]
    ang = jnp.concatenate([ang_r, ang_r, ang_c, ang_c], axis=-1)
    return jnp.cos(ang), jnp.sin(ang)


def _rotate_half(z):
    z1, z2 = jnp.split(z, 2, axis=-1)
    return jnp.concatenate([-z2, z1], axis=-1)


def apply_axial_rope(x, cos, sin):
    xf = x.astype(F32)
    half = HEAD_DIM // 2
    rot = jnp.concatenate([_rotate_half(xf[..., :half]), _rotate_half(xf[..., half:])], axis=-1)
    return (xf * cos[:, None, :] + rot * sin[:, None, :]).astype(x.dtype)


def gqa_attention(q, k, v):
    B, L = q.shape[0], q.shape[1]
    G = N_HEADS // N_KV_HEADS
    nb = L // Q_BLOCK
    scale = HEAD_DIM ** -0.5
    qb = q.reshape(B, nb, Q_BLOCK, N_KV_HEADS, G, HEAD_DIM).transpose(1, 0, 2, 3, 4, 5)

    def block(q_blk):
        s = jnp.einsum('bqkgd,bskd->bkgqs', q_blk, k, preferred_element_type=F32) * scale
        p = jax.nn.softmax(s, axis=-1).astype(v.dtype)
        return jnp.einsum('bkgqs,bskd->bqkgd', p, v)

    o = lax.map(block, qb)
    return o.transpose(1, 0, 2, 3, 4, 5).reshape(B, L, N_HEADS * HEAD_DIM)


def hyena_filters(L, w1, b1, f1, w2, b2, f2, w3):
    pos = jnp.arange(L, dtype=F32)
    t = jnp.linspace(0.0, 1.0, L, dtype=F32)
    bands = jnp.arange(1, POS_BANDS + 1, dtype=F32)
    ang = (2.0 * math.pi / L) * pos[:, None] * bands[None, :]
    z = jnp.concatenate([t[:, None], jnp.cos(ang), jnp.sin(ang)], axis=-1)
    h = jnp.sin(f1.astype(F32) * (z @ w1.astype(F32) + b1.astype(F32)))
    h = jnp.sin(f2.astype(F32) * (h @ w2.astype(F32) + b2.astype(F32)))
    h = h @ w3.astype(F32)
    max_decay = math.log(DECAY_TARGET) / DECAY_FAST_PCT
    min_decay = math.log(DECAY_TARGET) / DECAY_SLOW_PCT
    deltas = jnp.linspace(min_decay, max_decay, D_HYENA, dtype=F32)
    decay = jnp.exp(-t[:, None] * jnp.abs(deltas)[None, :])
    hf = h[:, :D_HYENA] * decay
    hb = h[:, D_HYENA:] * decay
    kern = jnp.concatenate([hf[:1] + hb[:1], hf[1:], jnp.zeros((1, D_HYENA), F32), hb[:0:-1]], axis=0)
    return kern / jnp.sum(jnp.abs(kern), axis=0, keepdims=True)


def hyena_mix(u, conv_w, conv_b, kern, filt_bias):
    L = u.shape[1]
    pad = SHORT_CONV // 2
    up = jnp.pad(u, ((0, 0), (pad, pad), (0, 0)))
    uc = sum(up[:, i:i + L] * conv_w[i] for i in range(SHORT_CONV)) + conv_b
    x0, x1, v = jnp.split(uc, 3, axis=-1)
    z = (v * x1).astype(F32)
    Z = jnp.fft.rfft(z, n=2 * L, axis=1)
    K = jnp.fft.rfft(kern, n=2 * L, axis=0)
    y = jnp.fft.irfft(Z * K[None], n=2 * L, axis=1)[:, :L]
    y = y + z * filt_bias.astype(F32)
    return y.astype(u.dtype) * x0


def memory_cross_attention(h, mem_n, w_cq, w_ckv, cq_norm, ck_norm, w_co):
    B, L, _ = h.shape
    M = mem_n.shape[1]
    q = (h @ w_cq).reshape(B, L, N_XHEADS, XHEAD_DIM)
    k, v = jnp.split(mem_n @ w_ckv, 2, axis=-1)
    k = k.reshape(B, M, N_XHEADS, XHEAD_DIM)
    v = v.reshape(B, M, N_XHEADS, XHEAD_DIM)
    q = rms_norm(q, cq_norm)
    k = rms_norm(k, ck_norm)
    s = jnp.einsum('blhd,bmhd->bhlm', q, k, preferred_element_type=F32) * (XHEAD_DIM ** -0.5)
    p = jax.nn.softmax(s, axis=-1).astype(v.dtype)
    o = jnp.einsum('bhlm,bmhd->blhd', p, v).reshape(B, L, D_MODEL)
    return o @ w_co


def expert_choice_ffn(h, w_router, w_gate, w_up, w_down):
    B, L, D = h.shape
    N = B * L
    C = (EC_CAPACITY * N) // N_EXPERTS
    xt = h.reshape(N, D)
    aff = jax.nn.softmax(jnp.einsum('nd,de->ne', xt, w_router, preferred_element_type=F32), axis=-1)
    g, idx = lax.top_k(aff.T, C)
    xe = xt[idx]
    a = jnp.einsum('ecd,edf->ecf', xe, w_gate)
    b = jnp.einsum('ecd,edf->ecf', xe, w_up)
    ye = jnp.einsum('ecf,efd->ecd', jax.nn.silu(a) * b, w_down)
    ye = ye * g[..., None].astype(ye.dtype)
    out = jnp.zeros_like(xt).at[idx.reshape(-1)].add(ye.reshape(-1, D))
    return out.reshape(B, L, D)


def run_trunk(x, mem, weights):
    (g_mix, w_in, q_norm, k_norm, conv_w, conv_b, filt_w1, filt_b1, filt_freq1,
     filt_w2, filt_b2, filt_freq2, filt_w3, filt_bias, w_attn_o, w_hyena_o, w_out,
     g_cross, g_mem, w_cq, w_ckv, cq_norm, ck_norm, w_co,
     g_ffn, w_router, w_gate, w_up, w_down) = weights
    B, L, _ = x.shape
    cos, sin = axial_rope_tables(L)
    cuts = [Q_W, Q_W + KV_W, Q_W + 2 * KV_W, Q_W + 2 * KV_W + HY_W]
    for l in range(DEPTH):
        h = rms_norm(x, g_mix[l])
        proj = h @ w_in[l]
        q, k, v, hy, gates = jnp.split(proj, cuts, axis=-1)
        q = rms_norm(q.reshape(B, L, N_HEADS, HEAD_DIM), q_norm[l])
        k = rms_norm(k.reshape(B, L, N_KV_HEADS, HEAD_DIM), k_norm[l])
        v = v.reshape(B, L, N_KV_HEADS, HEAD_DIM)
        q = apply_axial_rope(q, cos, sin)
        k = apply_axial_rope(k, cos, sin)
        attn = gqa_attention(q, k, v) @ w_attn_o[l]
        kern = hyena_filters(L, filt_w1[l], filt_b1[l], filt_freq1[l], filt_w2[l], filt_b2[l], filt_freq2[l], filt_w3[l])
        hyo = hyena_mix(hy, conv_w[l], conv_b[l], kern, filt_bias[l]) @ w_hyena_o[l]
        g_a, g_b = jnp.split(jax.nn.sigmoid(gates.astype(F32)), 2, axis=-1)
        merged = (g_a * attn.astype(F32) + g_b * hyo.astype(F32)).astype(x.dtype)
        x = x + merged @ w_out[l]
        x = x + memory_cross_attention(rms_norm(x, g_cross[l]), rms_norm(mem, g_mem[l]),
                                       w_cq[l], w_ckv[l], cq_norm[l], ck_norm[l], w_co[l])
        x = x + expert_choice_ffn(rms_norm(x, g_ffn[l]), w_router[l], w_gate[l], w_up[l], w_down[l])
    return x


def setup_inputs(seed: int = 0) -> dict:
    key = jax.random.key(seed)
    ks = jax.random.split(key, 40)

    def nrm(k, shape, scale):
        return jax.random.normal(k, shape, F32) * scale

    def gain(k, shape, noise=0.02):
        return 1.0 + noise * jax.random.normal(k, shape, F32)

    return {
        'x_prompt': nrm(ks[0], (BATCH, SEQ, D_MODEL), 1.0),
        'x_sample': nrm(ks[1], (DEC_BATCH, DEC_SEQ, D_MODEL), 1.0),
        'mem_prompt': nrm(ks[2], (BATCH, N_MEM, D_MODEL), 1.0),
        'mem_sample': nrm(ks[3], (DEC_BATCH, N_MEM, D_MODEL), 1.0),
        'g_mix': gain(ks[4], (DEPTH, D_MODEL)),
        'w_in': nrm(ks[5], (DEPTH, D_MODEL, D_IN), D_MODEL ** -0.5),
        'q_norm': gain(ks[6], (DEPTH, HEAD_DIM)),
        'k_norm': gain(ks[7], (DEPTH, HEAD_DIM)),
        'conv_w': nrm(ks[8], (DEPTH, SHORT_CONV, HY_W), SHORT_CONV ** -0.5),
        'conv_b': nrm(ks[9], (DEPTH, HY_W), 0.02),
        'filt_w1': nrm(ks[10], (DEPTH, POS_DIM, FILTER_HIDDEN), POS_DIM ** -0.5),
        'filt_b1': nrm(ks[11], (DEPTH, FILTER_HIDDEN), 0.02),
        'filt_freq1': gain(ks[12], (DEPTH, FILTER_HIDDEN), 0.1),
        'filt_w2': nrm(ks[13], (DEPTH, FILTER_HIDDEN, FILTER_HIDDEN), FILTER_HIDDEN ** -0.5),
        'filt_b2': nrm(ks[14], (DEPTH, FILTER_HIDDEN), 0.02),
        'filt_freq2': gain(ks[15], (DEPTH, FILTER_HIDDEN), 0.1),
        'filt_w3': nrm(ks[16], (DEPTH, FILTER_HIDDEN, 2 * D_HYENA), FILTER_HIDDEN ** -0.5),
        'filt_bias': nrm(ks[17], (DEPTH, D_HYENA), 0.2),
        'w_attn_o': nrm(ks[18], (DEPTH, Q_W, D_MODEL), Q_W ** -0.5),
        'w_hyena_o': nrm(ks[19], (DEPTH, D_HYENA, D_MODEL), D_HYENA ** -0.5),
        'w_out': nrm(ks[20], (DEPTH, D_MODEL, D_MODEL), D_MODEL ** -0.5),
        'g_cross': gain(ks[21], (DEPTH, D_MODEL)),
        'g_mem': gain(ks[22], (DEPTH, D_MODEL)),
        'w_cq': nrm(ks[23], (DEPTH, D_MODEL, D_MODEL), D_MODEL ** -0.5),
        'w_ckv': nrm(ks[24], (DEPTH, D_MODEL, 2 * D_MODEL), D_MODEL ** -0.5),
        'cq_norm': gain(ks[25], (DEPTH, XHEAD_DIM)),
        'ck_norm': gain(ks[26], (DEPTH, XHEAD_DIM)),
        'w_co': nrm(ks[27], (DEPTH, D_MODEL, D_MODEL), D_MODEL ** -0.5),
        'g_ffn': gain(ks[28], (DEPTH, D_MODEL)),
        'w_router': nrm(ks[29], (DEPTH, D_MODEL, N_EXPERTS), D_MODEL ** -0.5),
        'w_gate': nrm(ks[30], (DEPTH, N_EXPERTS, D_MODEL, D_EXPERT), D_MODEL ** -0.5),
        'w_up': nrm(ks[31], (DEPTH, N_EXPERTS, D_MODEL, D_EXPERT), D_MODEL ** -0.5),
        'w_down': nrm(ks[32], (DEPTH, N_EXPERTS, D_EXPERT, D_MODEL), D_EXPERT ** -0.5),
    }


def reference(x_prompt, x_sample, mem_prompt, mem_sample, g_mix, w_in, q_norm, k_norm,
              conv_w, conv_b, filt_w1, filt_b1, filt_freq1, filt_w2, filt_b2, filt_freq2,
              filt_w3, filt_bias, w_attn_o, w_hyena_o, w_out, g_cross, g_mem, w_cq, w_ckv,
              cq_norm, ck_norm, w_co, g_ffn, w_router, w_gate, w_up, w_down):
    weights = (g_mix, w_in, q_norm, k_norm, conv_w, conv_b, filt_w1, filt_b1, filt_freq1,
               filt_w2, filt_b2, filt_freq2, filt_w3, filt_bias, w_attn_o, w_hyena_o, w_out,
               g_cross, g_mem, w_cq, w_ckv, cq_norm, ck_norm, w_co,
               g_ffn, w_router, w_gate, w_up, w_down)
    y_prompt = run_trunk(x_prompt, mem_prompt, weights)
    y_sample = run_trunk(x_sample, mem_sample, weights)
    return (y_prompt, y_sample)
```

```python
import functools
import math

import jax
import jax.numpy as jnp
from jax import lax
from jax.experimental import pallas as pl
from jax.experimental.pallas import tpu as pltpu

F32 = jnp.float32
BF16 = jnp.bfloat16

N_HEADS = 8
N_KV_HEADS = 2
HEAD_DIM = 128
ROPE_THETA = 10000.0
GRID_W = 64
D_HYENA = 512
SHORT_CONV = 3
POS_BANDS = 16
DECAY_FAST_PCT = 0.3
DECAY_SLOW_PCT = 1.5
DECAY_TARGET = 1e-2
N_XHEADS = 4
N_EXPERTS = 16
EC_CAPACITY = 2
EPS = 1e-6

VMEM_LIMIT_BYTES = 56 * 1024 * 1024
LANES = 128


def _params(*sem):
    return pltpu.CompilerParams(dimension_semantics=sem, vmem_limit_bytes=VMEM_LIMIT_BYTES)


def _rms(x, gain):
    ms = jnp.mean(x * x, axis=-1, keepdims=True)
    return x * lax.rsqrt(ms + EPS) * gain


IN_TN = 512


def _in_proj_kernel(x_ref, g_ref, w_ref, cos_ref, sa_ref, sb_ref, qn_ref, kn_ref,
                    qkv_ref, hy_ref, gate_ref, h_scr, *, n_q, n_qkv, n_hy):
    j = pl.program_id(1)

    @pl.when(j == 0)
    def _():
        h_scr[...] = _rms(x_ref[...], g_ref[...]).astype(BF16)

    acc = jnp.dot(h_scr[...], w_ref[...], preferred_element_type=F32)

    def norm_rope(t, gain):
        y = _rms(t, gain)
        up = pltpu.roll(y, 3 * HEAD_DIM // 4, 1)
        dn = pltpu.roll(y, HEAD_DIM // 4, 1)
        return y * cos_ref[...] + up * sa_ref[...] + dn * sb_ref[...]

    heads = IN_TN // HEAD_DIM

    @pl.when(j < n_q)
    def _():
        for h in range(heads):
            sl = slice(h * HEAD_DIM, (h + 1) * HEAD_DIM)
            qkv_ref[:, sl] = norm_rope(acc[:, sl], qn_ref[...]).astype(BF16)

    @pl.when(j == n_q)
    def _():
        for h in range(heads):
            sl = slice(h * HEAD_DIM, (h + 1) * HEAD_DIM)
            if h < N_KV_HEADS:
                qkv_ref[:, sl] = norm_rope(acc[:, sl], kn_ref[...]).astype(BF16)
            else:
                qkv_ref[:, sl] = acc[:, sl].astype(BF16)

    @pl.when((j >= n_qkv) & (j < n_qkv + n_hy))
    def _():
        hy_ref[...] = acc

    @pl.when(j >= n_qkv + n_hy)
    def _():
        gate_ref[...] = (1.0 / (1.0 + jnp.exp(-acc))).astype(BF16)


def _in_proj(x2, g, w_in, cos, sa, sb, qn, kn, seq_len, tm=1024):
    n, d = x2.shape
    d_in = w_in.shape[1]
    q_w = N_HEADS * HEAD_DIM
    kv_w = N_KV_HEADS * HEAD_DIM
    hy_w = 3 * D_HYENA
    gate_w = d_in - q_w - 2 * kv_w - hy_w
    assert 2 * kv_w == IN_TN and q_w % IN_TN == 0 and hy_w % IN_TN == 0 and gate_w % IN_TN == 0
    n_q = q_w // IN_TN
    n_qkv = n_q + 1
    n_hy = hy_w // IN_TN
    n_gate = gate_w // IN_TN
    tm = min(tm, seq_len)
    assert seq_len % tm == 0 and n % tm == 0
    tiles_per_seq = seq_len // tm
    grid = (n // tm, n_qkv + n_hy + n_gate)
    tab = pl.BlockSpec((tm, HEAD_DIM), lambda i, j: (i % tiles_per_seq, 0))
    vec = lambda width: pl.BlockSpec((1, width), lambda i, j: (0, 0))
    return pl.pallas_call(
        functools.partial(_in_proj_kernel, n_q=n_q, n_qkv=n_qkv, n_hy=n_hy),
        grid=grid,
        in_specs=[
            pl.BlockSpec((tm, d), lambda i, j: (i, 0)),
            vec(d),
            pl.BlockSpec((d, IN_TN), lambda i, j: (0, j)),
            tab, tab, tab,
            vec(HEAD_DIM), vec(HEAD_DIM),
        ],
        out_specs=[
            pl.BlockSpec((tm, IN_TN), lambda i, j: (i, jnp.minimum(j, n_qkv - 1))),
            pl.BlockSpec((tm, IN_TN), lambda i, j: (i, jnp.clip(j - n_qkv, 0, n_hy - 1))),
            pl.BlockSpec((tm, IN_TN), lambda i, j: (i, jnp.clip(j - n_qkv - n_hy, 0, n_gate - 1))),
        ],
        out_shape=[
            jax.ShapeDtypeStruct((n, q_w + 2 * kv_w), BF16),
            jax.ShapeDtypeStruct((n, hy_w), F32),
            jax.ShapeDtypeStruct((n, gate_w), BF16),
        ],
        scratch_shapes=[pltpu.VMEM((tm, d), BF16)],
        compiler_params=_params("parallel", "arbitrary"),
        name="in_proj",
    )(x2, g, w_in, cos, sa, sb, qn, kn)


def _attn_kernel(q_ref, k_ref, v_ref, o_ref, m_scr, l_scr, acc_scr, *, tk, log2_scale):
    group = N_HEADS // N_KV_HEADS
    seq = k_ref.shape[1]
    m_scr[...] = jnp.full(m_scr.shape, -jnp.inf, F32)
    l_scr[...] = jnp.zeros(l_scr.shape, F32)
    acc_scr[...] = jnp.zeros(acc_scr.shape, F32)

    def body(kc, carry):
        off = pl.multiple_of(kc * tk, tk)
        k = k_ref[0, pl.ds(off, tk), :]
        v = v_ref[0, pl.ds(off, tk), :]
        for h in range(group):
            q = q_ref[0, :, h * HEAD_DIM:(h + 1) * HEAD_DIM]
            s = lax.dot_general(q, k, (((1,), (1,)), ((), ())), preferred_element_type=F32) * log2_scale
            m_prev = m_scr[h]
            m_new = jnp.maximum(m_prev, jnp.max(s, axis=-1, keepdims=True))
            alpha = jnp.exp2(m_prev - m_new)
            p = jnp.exp2(s - m_new)
            l_scr[h] = alpha * l_scr[h] + jnp.sum(p, axis=-1, keepdims=True)
            acc_scr[h] = alpha * acc_scr[h] + jnp.dot(p.astype(BF16), v, preferred_element_type=F32)
            m_scr[h] = m_new
        return carry

    lax.fori_loop(0, seq // tk, body, 0)
    for h in range(group):
        o_ref[0, :, h * HEAD_DIM:(h + 1) * HEAD_DIM] = (acc_scr[h] / l_scr[h]).astype(BF16)


def _attention(qkv3, tq=256, tk=512):
    b, seq, _ = qkv3.shape
    group = N_HEADS // N_KV_HEADS
    gw = group * HEAD_DIM
    tq = min(tq, seq)
    tk = min(tk, seq)
    log2_scale = (HEAD_DIM ** -0.5) * math.log2(math.e)
    k_blk0 = N_HEADS
    v_blk0 = N_HEADS + N_KV_HEADS
    return pl.pallas_call(
        functools.partial(_attn_kernel, tk=tk, log2_scale=log2_scale),
        grid=(b, N_KV_HEADS, seq // tq),
        in_specs=[
            pl.BlockSpec((1, tq, gw), lambda bi, kv, qi: (bi, qi, kv)),
            pl.BlockSpec((1, seq, HEAD_DIM), lambda bi, kv, qi: (bi, 0, k_blk0 + kv)),
            pl.BlockSpec((1, seq, HEAD_DIM), lambda bi, kv, qi: (bi, 0, v_blk0 + kv)),
        ],
        out_specs=pl.BlockSpec((1, tq, gw), lambda bi, kv, qi: (bi, qi, kv)),
        out_shape=jax.ShapeDtypeStruct((b, seq, N_HEADS * HEAD_DIM), BF16),
        scratch_shapes=[
            pltpu.VMEM((group, tq, 1), F32),
            pltpu.VMEM((group, tq, 1), F32),
            pltpu.VMEM((group, tq, HEAD_DIM), F32),
        ],
        compiler_params=_params("parallel", "parallel", "arbitrary"),
        name="gqa_attention",
    )(qkv3, qkv3, qkv3)


def _mix_kernel(o_ref, y_ref, gate_ref, x_ref, wao_ref, who_ref, wout_ref, out_ref):
    d = x_ref.shape[1]
    attn = jnp.dot(o_ref[...], wao_ref[...], preferred_element_type=F32)
    hyo = jnp.dot(y_ref[...].astype(BF16), who_ref[...], preferred_element_type=F32)
    ga = gate_ref[:, :d].astype(F32)
    gb = gate_ref[:, d:].astype(F32)
    merged = (ga * attn + gb * hyo).astype(BF16)
    out_ref[...] = x_ref[...] + jnp.dot(merged, wout_ref[...], preferred_element_type=F32)


def _mix(o2, y2, gate2, x2, wao, who, wout, tm=512):
    n, d = x2.shape
    tm = min(tm, n)
    row = lambda width: pl.BlockSpec((tm, width), lambda i: (i, 0))
    full = lambda a: pl.BlockSpec(a.shape, lambda i: (0, 0))
    return pl.pallas_call(
        _mix_kernel,
        grid=(n // tm,),
        in_specs=[row(o2.shape[1]), row(y2.shape[1]), row(gate2.shape[1]), row(d),
                  full(wao), full(who), full(wout)],
        out_specs=row(d),
        out_shape=jax.ShapeDtypeStruct((n, d), F32),
        compiler_params=_params("parallel"),
        name="mix_out",
    )(o2, y2, gate2, x2, wao, who, wout)


def _memkv_kernel(m_ref, g_ref, w_ref, kn_ref, out_ref, h_scr):
    j = pl.program_id(1)

    @pl.when(j == 0)
    def _():
        h_scr[...] = _rms(m_ref[...], g_ref[...]).astype(BF16)

    acc = jnp.dot(h_scr[...], w_ref[...], preferred_element_type=F32)

    @pl.when(j < N_XHEADS)
    def _():
        out_ref[...] = _rms(acc, kn_ref[...]).astype(BF16)

    @pl.when(j >= N_XHEADS)
    def _():
        out_ref[...] = acc.astype(BF16)


def _mem_kv(mem2, g, w_ckv, kn, tm=256):
    n, d = mem2.shape
    xd = d // N_XHEADS
    tm = min(tm, n)
    return pl.pallas_call(
        _memkv_kernel,
        grid=(n // tm, 2 * N_XHEADS),
        in_specs=[
            pl.BlockSpec((tm, d), lambda i, j: (i, 0)),
            pl.BlockSpec((1, d), lambda i, j: (0, 0)),
            pl.BlockSpec((d, xd), lambda i, j: (0, j)),
            pl.BlockSpec((1, xd), lambda i, j: (0, 0)),
        ],
        out_specs=pl.BlockSpec((tm, xd), lambda i, j: (i, j)),
        out_shape=jax.ShapeDtypeStruct((n, 2 * d), BF16),
        scratch_shapes=[pltpu.VMEM((tm, d), BF16)],
        compiler_params=_params("parallel", "arbitrary"),
        name="mem_kv",
    )(mem2, g, w_ckv, kn)


def _cross_kernel(x_ref, g_ref, wq_ref, qn_ref, kv_ref, wo_ref, out_ref, o_scr):
    d = x_ref.shape[2]
    xd = d // N_XHEADS
    x = x_ref[0]
    h = _rms(x, g_ref[...]).astype(BF16)
    q = jnp.dot(h, wq_ref[...], preferred_element_type=F32)
    scale = xd ** -0.5
    for hd in range(N_XHEADS):
        sl = slice(hd * xd, (hd + 1) * xd)
        qh = _rms(q[:, sl], qn_ref[...]).astype(BF16)
        kh = kv_ref[0, :, sl]
        vh = kv_ref[0, :, d + hd * xd:d + (hd + 1) * xd]
        s = lax.dot_general(qh, kh, (((1,), (1,)), ((), ())), preferred_element_type=F32) * scale
        m = jnp.max(s, axis=-1, keepdims=True)
        p = jnp.exp(s - m)
        l = jnp.sum(p, axis=-1, keepdims=True)
        o = jnp.dot(p.astype(BF16), vh, preferred_element_type=F32) / l
        o_scr[:, sl] = o.astype(BF16)
    out_ref[0] = x + jnp.dot(o_scr[...], wo_ref[...], preferred_element_type=F32)


def _cross(x3, g, wq, qn, kv3, wo, tm=512):
    b, seq, d = x3.shape
    xd = d // N_XHEADS
    n_mem = kv3.shape[1]
    tm = min(tm, seq)
    return pl.pallas_call(
        _cross_kernel,
        grid=(b, seq // tm),
        in_specs=[
            pl.BlockSpec((1, tm, d), lambda bi, i: (bi, i, 0)),
            pl.BlockSpec((1, d), lambda bi, i: (0, 0)),
            pl.BlockSpec((d, d), lambda bi, i: (0, 0)),
            pl.BlockSpec((1, xd), lambda bi, i: (0, 0)),
            pl.BlockSpec((1, n_mem, 2 * d), lambda bi, i: (bi, 0, 0)),
            pl.BlockSpec((d, d), lambda bi, i: (0, 0)),
        ],
        out_specs=pl.BlockSpec((1, tm, d), lambda bi, i: (bi, i, 0)),
        out_shape=jax.ShapeDtypeStruct((b, seq, d), F32),
        scratch_shapes=[pltpu.VMEM((tm, d), BF16)],
        compiler_params=_params("parallel", "parallel"),
        name="cross_attention",
    )(x3, g, wq, qn, kv3, wo)


def _ffn_pre_kernel(x_ref, g_ref, wr_ref, hb_ref, aff_ref):
    h = _rms(x_ref[...], g_ref[...])
    hb_ref[...] = h.astype(BF16)
    logits = lax.dot_general(wr_ref[...], h, (((1,), (1,)), ((), ())),
                             precision=lax.Precision.HIGHEST, preferred_element_type=F32)
    m = jnp.max(logits, axis=0, keepdims=True)
    e = jnp.exp(logits - m)
    aff_ref[...] = e / jnp.sum(e, axis=0, keepdims=True)


def _ffn_pre(x2, g, wr_t, tm=512):
    n, d = x2.shape
    e = wr_t.shape[0]
    tm = min(tm, n)
    return pl.pallas_call(
        _ffn_pre_kernel,
        grid=(n // tm,),
        in_specs=[
            pl.BlockSpec((tm, d), lambda i: (i, 0)),
            pl.BlockSpec((1, d), lambda i: (0, 0)),
            pl.BlockSpec((e, d), lambda i: (0, 0)),
        ],
        out_specs=[
            pl.BlockSpec((tm, d), lambda i: (i, 0)),
            pl.BlockSpec((e, tm), lambda i: (0, i)),
        ],
        out_shape=[
            jax.ShapeDtypeStruct((n, d), BF16),
            jax.ShapeDtypeStruct((e, n), F32),
        ],
        compiler_params=_params("parallel"),
        name="ffn_pre",
    )(x2, g, wr_t)


def _expert_kernel(xe_ref, wg_ref, wu_ref, wd_ref, g_ref, out_ref):
    f = pl.program_id(2)
    x = xe_ref[0]
    a = jnp.dot(x, wg_ref[0].astype(BF16), preferred_element_type=F32)
    b = jnp.dot(x, wu_ref[0].astype(BF16), preferred_element_type=F32)
    mid = (a * (1.0 / (1.0 + jnp.exp(-a))) * b).astype(BF16)
    y = jnp.dot(mid, wd_ref[0].astype(BF16), preferred_element_type=F32)

    @pl.when(f == 0)
    def _():
        out_ref[0] = y

    @pl.when(f > 0)
    def _():
        out_ref[0] += y

    @pl.when(f == pl.num_programs(2) - 1)
    def _():
        out_ref[0] = out_ref[0] * g_ref[0]


def _experts(xe, w_gate, w_up, w_down, g3, tc=2048, tf=256):
    e, c, d = xe.shape
    f = w_gate.shape[2]
    tc = min(tc, c)
    tf = min(tf, f)
    return pl.pallas_call(
        _expert_kernel,
        grid=(e, c // tc, f // tf),
        in_specs=[
            pl.BlockSpec((1, tc, d), lambda ei, ci, fi: (ei, ci, 0)),
            pl.BlockSpec((1, d, tf), lambda ei, ci, fi: (ei, 0, fi)),
            pl.BlockSpec((1, d, tf), lambda ei, ci, fi: (ei, 0, fi)),
            pl.BlockSpec((1, tf, d), lambda ei, ci, fi: (ei, fi, 0)),
            pl.BlockSpec((1, tc, 1), lambda ei, ci, fi: (ei, ci, 0)),
        ],
        out_specs=pl.BlockSpec((1, tc, d), lambda ei, ci, fi: (ei, ci, 0)),
        out_shape=jax.ShapeDtypeStruct((e, c, d), F32),
        compiler_params=_params("parallel", "parallel", "arbitrary"),
        name="experts",
    )(xe, w_gate, w_up, w_down, g3)


def _rope_tables(seq):
    rows_n = seq // GRID_W
    row = jnp.repeat(jnp.arange(rows_n, dtype=F32), GRID_W)
    col = jnp.tile(jnp.arange(GRID_W, dtype=F32), rows_n)
    half = HEAD_DIM // 2
    inv = ROPE_THETA ** (-jnp.arange(0, half, 2, dtype=F32) / half)
    ang_r = row[:, None] * inv[None, :]
    ang_c = col[:, None] * inv[None, :]
    ang = jnp.concatenate([ang_r, ang_r, ang_c, ang_c], axis=-1)
    cos, sin = jnp.cos(ang), jnp.sin(ang)
    lower = (jnp.arange(HEAD_DIM) % half) < (half // 2)
    sa = jnp.where(lower[None, :], -sin, 0.0)
    sb = jnp.where(lower[None, :], 0.0, sin)
    return cos, sa, sb


def _hyena_filters(seq, w1, b1, f1, w2, b2, f2, w3):
    pos = jnp.arange(seq, dtype=F32)
    t = jnp.linspace(0.0, 1.0, seq, dtype=F32)
    bands = jnp.arange(1, POS_BANDS + 1, dtype=F32)
    ang = (2.0 * math.pi / seq) * pos[:, None] * bands[None, :]
    z = jnp.concatenate([t[:, None], jnp.cos(ang), jnp.sin(ang)], axis=-1)
    hp = lax.Precision.HIGHEST
    h = jnp.sin(f1 * (jnp.dot(z, w1, precision=hp) + b1))
    h = jnp.sin(f2 * (jnp.dot(h, w2, precision=hp) + b2))
    h = jnp.dot(h, w3, precision=hp)
    max_decay = math.log(DECAY_TARGET) / DECAY_FAST_PCT
    min_decay = math.log(DECAY_TARGET) / DECAY_SLOW_PCT
    deltas = jnp.linspace(min_decay, max_decay, D_HYENA, dtype=F32)
    decay = jnp.exp(-t[:, None] * jnp.abs(deltas)[None, :])
    hf = h[:, :D_HYENA] * decay
    hb = h[:, D_HYENA:] * decay
    kern = jnp.concatenate([hf[:1] + hb[:1], hf[1:], jnp.zeros((1, D_HYENA), F32), hb[:0:-1]], axis=0)
    return kern / jnp.sum(jnp.abs(kern), axis=0, keepdims=True)


def _hyena_mix(u, conv_w, conv_b, kern, filt_bias):
    seq = u.shape[1]
    pad = SHORT_CONV // 2
    up = jnp.pad(u, ((0, 0), (pad, pad), (0, 0)))
    uc = sum(up[:, i:i + seq] * conv_w[i] for i in range(SHORT_CONV)) + conv_b
    x0, x1, v = jnp.split(uc, 3, axis=-1)
    z = v * x1
    zf = jnp.fft.rfft(z, n=2 * seq, axis=1)
    kf = jnp.fft.rfft(kern, n=2 * seq, axis=0)
    y = jnp.fft.irfft(zf * kf[None], n=2 * seq, axis=1)[:, :seq]
    y = y + z * filt_bias
    return y * x0


def _run_trunk(x, mem, wts):
    b, seq, d = x.shape
    n = b * seq
    n_mem = mem.shape[1]
    depth = wts["w_in"].shape[0]
    cos, sa, sb = _rope_tables(seq)
    cap = (EC_CAPACITY * n) // N_EXPERTS
    mem2 = mem.reshape(b * n_mem, d)
    x2 = x.reshape(n, d)
    row = lambda a: a.reshape(1, -1)
    for l in range(depth):
        qkv, hy, gates = _in_proj(x2, row(wts["g_mix"][l]), wts["w_in"][l], cos, sa, sb,
                                  row(wts["q_norm"][l]), row(wts["k_norm"][l]), seq)
        o = _attention(qkv.reshape(b, seq, -1))
        kern = _hyena_filters(seq, wts["filt_w1"][l], wts["filt_b1"][l], wts["filt_freq1"][l],
                              wts["filt_w2"][l], wts["filt_b2"][l], wts["filt_freq2"][l], wts["filt_w3"][l])
        y = _hyena_mix(hy.reshape(b, seq, -1), wts["conv_w"][l], wts["conv_b"][l], kern, wts["filt_bias"][l])
        x2 = _mix(o.reshape(n, -1), y.reshape(n, -1), gates, x2,
                  wts["w_attn_o"][l], wts["w_hyena_o"][l], wts["w_out"][l])
        kv = _mem_kv(mem2, row(wts["g_mem"][l]), wts["w_ckv"][l], row(wts["ck_norm"][l]))
        x3 = _cross(x2.reshape(b, seq, d), row(wts["g_cross"][l]), wts["w_cq"][l], row(wts["cq_norm"][l]),
                    kv.reshape(b, n_mem, 2 * d), wts["w_co"][l])
        x2 = x3.reshape(n, d)
        hb, aff_t = _ffn_pre(x2, row(wts["g_ffn"][l]), wts["w_router_t"][l])
        gsel, idx = lax.top_k(aff_t, cap)
        xe = hb[idx]
        ye = _experts(xe, wts["w_gate"][l], wts["w_up"][l], wts["w_down"][l], gsel[..., None])
        x2 = x2.at[idx.reshape(-1)].add(ye.reshape(-1, d))
    return x2.reshape(b, seq, d)


def kernel(x_prompt, x_sample, mem_prompt, mem_sample, g_mix, w_in, q_norm, k_norm, conv_w, conv_b,
           filt_w1, filt_b1, filt_freq1, filt_w2, filt_b2, filt_freq2, filt_w3, filt_bias, w_attn_o,
           w_hyena_o, w_out, g_cross, g_mem, w_cq, w_ckv, cq_norm, ck_norm, w_co, g_ffn, w_router,
           w_gate, w_up, w_down):
    wts = dict(
        g_mix=g_mix, w_in=w_in.astype(BF16), q_norm=q_norm, k_norm=k_norm, conv_w=conv_w, conv_b=conv_b,
        filt_w1=filt_w1, filt_b1=filt_b1, filt_freq1=filt_freq1, filt_w2=filt_w2, filt_b2=filt_b2,
        filt_freq2=filt_freq2, filt_w3=filt_w3, filt_bias=filt_bias,
        w_attn_o=w_attn_o.astype(BF16), w_hyena_o=w_hyena_o.astype(BF16), w_out=w_out.astype(BF16),
        g_cross=g_cross, g_mem=g_mem, w_cq=w_cq.astype(BF16), w_ckv=w_ckv.astype(BF16),
        cq_norm=cq_norm, ck_norm=ck_norm, w_co=w_co.astype(BF16), g_ffn=g_ffn,
        w_router_t=jnp.swapaxes(w_router, 1, 2), w_gate=w_gate, w_up=w_up, w_down=w_down,
    )
    y_prompt = _run_trunk(x_prompt, mem_prompt, wts)
    y_sample = _run_trunk(x_sample, mem_sample, wts)
    return (y_prompt, y_sample)
```

```python
import functools
import math

import jax
import jax.numpy as jnp
from jax import lax
from jax.experimental import pallas as pl
from jax.experimental.pallas import tpu as pltpu

F32 = jnp.float32
BF16 = jnp.bfloat16

N_HEADS = 8
N_KV_HEADS = 2
HEAD_DIM = 128
ROPE_THETA = 10000.0
GRID_W = 64
D_HYENA = 512
SHORT_CONV = 3
POS_BANDS = 16
DECAY_FAST_PCT = 0.3
DECAY_SLOW_PCT = 1.5
DECAY_TARGET = 1e-2
N_XHEADS = 4
N_EXPERTS = 16
EC_CAPACITY = 2
EPS = 1e-6

VMEM_LIMIT_BYTES = 56 * 1024 * 1024
LANES = 128


def _params(*sem):
    return pltpu.CompilerParams(dimension_semantics=sem, vmem_limit_bytes=VMEM_LIMIT_BYTES)


def _rms(x, gain):
    ms = jnp.mean(x * x, axis=-1, keepdims=True)
    return x * lax.rsqrt(ms + EPS) * gain


IN_TN = 512


def _in_proj_kernel(x_ref, g_ref, w_ref, cos_ref, sa_ref, sb_ref, qn_ref, kn_ref,
                    qkv_ref, hy_ref, gate_ref, h_scr, *, n_q, n_qkv, n_hy):
    j = pl.program_id(1)

    @pl.when(j == 0)
    def _():
        h_scr[...] = _rms(x_ref[...], g_ref[...]).astype(BF16)

    acc = jnp.dot(h_scr[...], w_ref[...], preferred_element_type=F32)

    def norm_rope(t, gain):
        y = _rms(t, gain)
        up = pltpu.roll(y, 3 * HEAD_DIM // 4, 1)
        dn = pltpu.roll(y, HEAD_DIM // 4, 1)
        return y * cos_ref[...] + up * sa_ref[...] + dn * sb_ref[...]

    heads = IN_TN // HEAD_DIM

    @pl.when(j < n_q)
    def _():
        for h in range(heads):
            sl = slice(h * HEAD_DIM, (h + 1) * HEAD_DIM)
            qkv_ref[:, sl] = norm_rope(acc[:, sl], qn_ref[...]).astype(BF16)

    @pl.when(j == n_q)
    def _():
        for h in range(heads):
            sl = slice(h * HEAD_DIM, (h + 1) * HEAD_DIM)
            if h < N_KV_HEADS:
                qkv_ref[:, sl] = norm_rope(acc[:, sl], kn_ref[...]).astype(BF16)
            else:
                qkv_ref[:, sl] = acc[:, sl].astype(BF16)

    @pl.when((j >= n_qkv) & (j < n_qkv + n_hy))
    def _():
        hy_ref[...] = acc

    @pl.when(j >= n_qkv + n_hy)
    def _():
        gate_ref[...] = (1.0 / (1.0 + jnp.exp(-acc))).astype(BF16)


def _in_proj(x2, g, w_in, cos, sa, sb, qn, kn, seq_len, tm=1024):
    n, d = x2.shape
    d_in = w_in.shape[1]
    q_w = N_HEADS * HEAD_DIM
    kv_w = N_KV_HEADS * HEAD_DIM
    hy_w = 3 * D_HYENA
    gate_w = d_in - q_w - 2 * kv_w - hy_w
    assert 2 * kv_w == IN_TN and q_w % IN_TN == 0 and hy_w % IN_TN == 0 and gate_w % IN_TN == 0
    n_q = q_w // IN_TN
    n_qkv = n_q + 1
    n_hy = hy_w // IN_TN
    n_gate = gate_w // IN_TN
    tm = min(tm, seq_len)
    assert seq_len % tm == 0 and n % tm == 0
    tiles_per_seq = seq_len // tm
    grid = (n // tm, n_qkv + n_hy + n_gate)
    tab = pl.BlockSpec((tm, HEAD_DIM), lambda i, j: (i % tiles_per_seq, 0))
    vec = lambda width: pl.BlockSpec((1, width), lambda i, j: (0, 0))
    return pl.pallas_call(
        functools.partial(_in_proj_kernel, n_q=n_q, n_qkv=n_qkv, n_hy=n_hy),
        grid=grid,
        in_specs=[
            pl.BlockSpec((tm, d), lambda i, j: (i, 0)),
            vec(d),
            pl.BlockSpec((d, IN_TN), lambda i, j: (0, j)),
            tab, tab, tab,
            vec(HEAD_DIM), vec(HEAD_DIM),
        ],
        out_specs=[
            pl.BlockSpec((tm, IN_TN), lambda i, j: (i, jnp.minimum(j, n_qkv - 1))),
            pl.BlockSpec((tm, IN_TN), lambda i, j: (i, jnp.clip(j - n_qkv, 0, n_hy - 1))),
            pl.BlockSpec((tm, IN_TN), lambda i, j: (i, jnp.clip(j - n_qkv - n_hy, 0, n_gate - 1))),
        ],
        out_shape=[
            jax.ShapeDtypeStruct((n, q_w + 2 * kv_w), BF16),
            jax.ShapeDtypeStruct((n, hy_w), F32),
            jax.ShapeDtypeStruct((n, gate_w), BF16),
        ],
        scratch_shapes=[pltpu.VMEM((tm, d), BF16)],
        compiler_params=_params("parallel", "arbitrary"),
        name="in_proj",
    )(x2, g, w_in, cos, sa, sb, qn, kn)


def _attn_kernel(q_ref, k_ref, v_ref, o_ref, m_scr, l_scr, acc_scr, *, tk, log2_scale):
    group = N_HEADS // N_KV_HEADS
    seq = k_ref.shape[1]
    m_scr[...] = jnp.full(m_scr.shape, -jnp.inf, F32)
    l_scr[...] = jnp.zeros(l_scr.shape, F32)
    acc_scr[...] = jnp.zeros(acc_scr.shape, F32)

    def body(kc, carry):
        off = pl.multiple_of(kc * tk, tk)
        k = k_ref[0, pl.ds(off, tk), :]
        v = v_ref[0, pl.ds(off, tk), :]
        for h in range(group):
            q = q_ref[0, :, h * HEAD_DIM:(h + 1) * HEAD_DIM]
            s = lax.dot_general(q, k, (((1,), (1,)), ((), ())), preferred_element_type=F32) * log2_scale
            m_prev = m_scr[h]
            m_new = jnp.maximum(m_prev, jnp.max(s, axis=-1, keepdims=True))
            alpha = jnp.exp2(m_prev - m_new)
            p = jnp.exp2(s - m_new)
            l_scr[h] = alpha * l_scr[h] + jnp.sum(p, axis=-1, keepdims=True)
            acc_scr[h] = alpha * acc_scr[h] + jnp.dot(p.astype(BF16), v, preferred_element_type=F32)
            m_scr[h] = m_new
        return carry

    lax.fori_loop(0, seq // tk, body, 0)
    for h in range(group):
        o_ref[0, :, h * HEAD_DIM:(h + 1) * HEAD_DIM] = (acc_scr[h] / l_scr[h]).astype(BF16)


def _attention(qkv3, tq=256, tk=512):
    b, seq, _ = qkv3.shape
    group = N_HEADS // N_KV_HEADS
    gw = group * HEAD_DIM
    tq = min(tq, seq)
    tk = min(tk, seq)
    log2_scale = (HEAD_DIM ** -0.5) * math.log2(math.e)
    k_blk0 = N_HEADS
    v_blk0 = N_HEADS + N_KV_HEADS
    return pl.pallas_call(
        functools.partial(_attn_kernel, tk=tk, log2_scale=log2_scale),
        grid=(b, N_KV_HEADS, seq // tq),
        in_specs=[
            pl.BlockSpec((1, tq, gw), lambda bi, kv, qi: (bi, qi, kv)),
            pl.BlockSpec((1, seq, HEAD_DIM), lambda bi, kv, qi: (bi, 0, k_blk0 + kv)),
            pl.BlockSpec((1, seq, HEAD_DIM), lambda bi, kv, qi: (bi, 0, v_blk0 + kv)),
        ],
        out_specs=pl.BlockSpec((1, tq, gw), lambda bi, kv, qi: (bi, qi, kv)),
        out_shape=jax.ShapeDtypeStruct((b, seq, N_HEADS * HEAD_DIM), BF16),
        scratch_shapes=[
            pltpu.VMEM((group, tq, 1), F32),
            pltpu.VMEM((group, tq, 1), F32),
            pltpu.VMEM((group, tq, HEAD_DIM), F32),
        ],
        compiler_params=_params("parallel", "parallel", "arbitrary"),
        name="gqa_attention",
    )(qkv3, qkv3, qkv3)


def _mix_kernel(o_ref, y_ref, gate_ref, x_ref, wao_ref, who_ref, wout_ref, out_ref):
    d = x_ref.shape[1]
    attn = jnp.dot(o_ref[...], wao_ref[...], preferred_element_type=F32)
    hyo = jnp.dot(y_ref[...].astype(BF16), who_ref[...], preferred_element_type=F32)
    ga = gate_ref[:, :d].astype(F32)
    gb = gate_ref[:, d:].astype(F32)
    merged = (ga * attn + gb * hyo).astype(BF16)
    out_ref[...] = x_ref[...] + jnp.dot(merged, wout_ref[...], preferred_element_type=F32)


def _mix(o2, y2, gate2, x2, wao, who, wout, tm=512):
    n, d = x2.shape
    tm = min(tm, n)
    row = lambda width: pl.BlockSpec((tm, width), lambda i: (i, 0))
    full = lambda a: pl.BlockSpec(a.shape, lambda i: (0, 0))
    return pl.pallas_call(
        _mix_kernel,
        grid=(n // tm,),
        in_specs=[row(o2.shape[1]), row(y2.shape[1]), row(gate2.shape[1]), row(d),
                  full(wao), full(who), full(wout)],
        out_specs=row(d),
        out_shape=jax.ShapeDtypeStruct((n, d), F32),
        compiler_params=_params("parallel"),
        name="mix_out",
    )(o2, y2, gate2, x2, wao, who, wout)


def _memkv_kernel(m_ref, g_ref, w_ref, kn_ref, out_ref, h_scr):
    j = pl.program_id(1)

    @pl.when(j == 0)
    def _():
        h_scr[...] = _rms(m_ref[...], g_ref[...]).astype(BF16)

    acc = jnp.dot(h_scr[...], w_ref[...], preferred_element_type=F32)

    @pl.when(j < N_XHEADS)
    def _():
        out_ref[...] = _rms(acc, kn_ref[...]).astype(BF16)

    @pl.when(j >= N_XHEADS)
    def _():
        out_ref[...] = acc.astype(BF16)


def _mem_kv(mem2, g, w_ckv, kn, tm=256):
    n, d = mem2.shape
    xd = d // N_XHEADS
    tm = min(tm, n)
    return pl.pallas_call(
        _memkv_kernel,
        grid=(n // tm, 2 * N_XHEADS),
        in_specs=[
            pl.BlockSpec((tm, d), lambda i, j: (i, 0)),
            pl.BlockSpec((1, d), lambda i, j: (0, 0)),
            pl.BlockSpec((d, xd), lambda i, j: (0, j)),
            pl.BlockSpec((1, xd), lambda i, j: (0, 0)),
        ],
        out_specs=pl.BlockSpec((tm, xd), lambda i, j: (i, j)),
        out_shape=jax.ShapeDtypeStruct((n, 2 * d), BF16),
        scratch_shapes=[pltpu.VMEM((tm, d), BF16)],
        compiler_params=_params("parallel", "arbitrary"),
        name="mem_kv",
    )(mem2, g, w_ckv, kn)


def _cross_kernel(x_ref, g_ref, wq_ref, qn_ref, kv_ref, wo_ref, out_ref, o_scr):
    d = x_ref.shape[2]
    xd = d // N_XHEADS
    x = x_ref[0]
    h = _rms(x, g_ref[...]).astype(BF16)
    q = jnp.dot(h, wq_ref[...], preferred_element_type=F32)
    scale = xd ** -0.5
    for hd in range(N_XHEADS):
        sl = slice(hd * xd, (hd + 1) * xd)
        qh = _rms(q[:, sl], qn_ref[...]).astype(BF16)
        kh = kv_ref[0, :, sl]
        vh = kv_ref[0, :, d + hd * xd:d + (hd + 1) * xd]
        s = lax.dot_general(qh, kh, (((1,), (1,)), ((), ())), preferred_element_type=F32) * scale
        m = jnp.max(s, axis=-1, keepdims=True)
        p = jnp.exp(s - m)
        l = jnp.sum(p, axis=-1, keepdims=True)
        o = jnp.dot(p.astype(BF16), vh, preferred_element_type=F32) / l
        o_scr[:, sl] = o.astype(BF16)
    out_ref[0] = x + jnp.dot(o_scr[...], wo_ref[...], preferred_element_type=F32)


def _cross(x3, g, wq, qn, kv3, wo, tm=512):
    b, seq, d = x3.shape
    xd = d // N_XHEADS
    n_mem = kv3.shape[1]
    tm = min(tm, seq)
    return pl.pallas_call(
        _cross_kernel,
        grid=(b, seq // tm),
        in_specs=[
            pl.BlockSpec((1, tm, d), lambda bi, i: (bi, i, 0)),
            pl.BlockSpec((1, d), lambda bi, i: (0, 0)),
            pl.BlockSpec((d, d), lambda bi, i: (0, 0)),
            pl.BlockSpec((1, xd), lambda bi, i: (0, 0)),
            pl.BlockSpec((1, n_mem, 2 * d), lambda bi, i: (bi, 0, 0)),
            pl.BlockSpec((d, d), lambda bi, i: (0, 0)),
        ],
        out_specs=pl.BlockSpec((1, tm, d), lambda bi, i: (bi, i, 0)),
        out_shape=jax.ShapeDtypeStruct((b, seq, d), F32),
        scratch_shapes=[pltpu.VMEM((tm, d), BF16)],
        compiler_params=_params("parallel", "parallel"),
        name="cross_attention",
    )(x3, g, wq, qn, kv3, wo)


def _ffn_pre_kernel(x_ref, g_ref, wr_ref, hb_ref, aff_ref):
    h = _rms(x_ref[...], g_ref[...])
    hb_ref[...] = h.astype(BF16)
    logits = lax.dot_general(wr_ref[...], h, (((1,), (1,)), ((), ())),
                             precision=lax.Precision.HIGHEST, preferred_element_type=F32)
    m = jnp.max(logits, axis=0, keepdims=True)
    e = jnp.exp(logits - m)
    aff_ref[...] = e / jnp.sum(e, axis=0, keepdims=True)


def _ffn_pre(x2, g, wr_t, tm=512):
    n, d = x2.shape
    e = wr_t.shape[0]
    tm = min(tm, n)
    return pl.pallas_call(
        _ffn_pre_kernel,
        grid=(n // tm,),
        in_specs=[
            pl.BlockSpec((tm, d), lambda i: (i, 0)),
            pl.BlockSpec((1, d), lambda i: (0, 0)),
            pl.BlockSpec((e, d), lambda i: (0, 0)),
        ],
        out_specs=[
            pl.BlockSpec((tm, d), lambda i: (i, 0)),
            pl.BlockSpec((e, tm), lambda i: (0, i)),
        ],
        out_shape=[
            jax.ShapeDtypeStruct((n, d), BF16),
            jax.ShapeDtypeStruct((e, n), F32),
        ],
        compiler_params=_params("parallel"),
        name="ffn_pre",
    )(x2, g, wr_t)


def _expert_kernel(xe_ref, wg_ref, wu_ref, wd_ref, g_ref, out_ref):
    f = pl.program_id(2)
    x = xe_ref[0]
    a = jnp.dot(x, wg_ref[0].astype(BF16), preferred_element_type=F32)
    b = jnp.dot(x, wu_ref[0].astype(BF16), preferred_element_type=F32)
    mid = (a * (1.0 / (1.0 + jnp.exp(-a))) * b).astype(BF16)
    y = jnp.dot(mid, wd_ref[0].astype(BF16), preferred_element_type=F32)

    @pl.when(f == 0)
    def _():
        out_ref[0] = y

    @pl.when(f > 0)
    def _():
        out_ref[0] += y

    @pl.when(f == pl.num_programs(2) - 1)
    def _():
        out_ref[0] = out_ref[0] * g_ref[0]


def _experts(xe, w_gate, w_up, w_down, g3, tc=2048, tf=256):
    e, c, d = xe.shape
    f = w_gate.shape[2]
    tc = min(tc, c)
    tf = min(tf, f)
    return pl.pallas_call(
        _expert_kernel,
        grid=(e, c // tc, f // tf),
        in_specs=[
            pl.BlockSpec((1, tc, d), lambda ei, ci, fi: (ei, ci, 0)),
            pl.BlockSpec((1, d, tf), lambda ei, ci, fi: (ei, 0, fi)),
            pl.BlockSpec((1, d, tf), lambda ei, ci, fi: (ei, 0, fi)),
            pl.BlockSpec((1, tf, d), lambda ei, ci, fi: (ei, fi, 0)),
            pl.BlockSpec((1, tc, 1), lambda ei, ci, fi: (ei, ci, 0)),
        ],
        out_specs=pl.BlockSpec((1, tc, d), lambda ei, ci, fi: (ei, ci, 0)),
        out_shape=jax.ShapeDtypeStruct((e, c, d), F32),
        compiler_params=_params("parallel", "parallel", "arbitrary"),
        name="experts",
    )(xe, w_gate, w_up, w_down, g3)


def _split_hl(x):
    hi = x.astype(BF16)
    lo = (x - hi.astype(F32)).astype(BF16)
    return hi, lo


def _stack3_rows(parts):
    his, los = zip(*[_split_hl(p) for p in parts])
    return jnp.concatenate(list(his) + list(los) + list(his), axis=0)


def _dot3(m_hi, m_lo, x):
    x_hi, x_lo = _split_hl(x)
    return (jnp.dot(m_hi, x_hi, preferred_element_type=F32)
            + jnp.dot(m_hi, x_lo, preferred_element_type=F32)
            + jnp.dot(m_lo, x_hi, preferred_element_type=F32))


def _hy_pre_kernel(u_ref, prev_ref, next_ref, w_ref, b_ref, z_ref, x0_ref):
    i = pl.program_id(1)
    u = u_ref[0]
    tl = u.shape[0]
    halo = prev_ref.shape[1]
    prev_row = jnp.where(i > 0, prev_ref[0, halo - 1:halo, :], 0.0)
    next_row = jnp.where(i < pl.num_programs(1) - 1, next_ref[0, 0:1, :], 0.0)
    rows = lax.broadcasted_iota(jnp.int32, u.shape, 0)
    u_prev = jnp.where(rows == 0, prev_row, pltpu.roll(u, 1, 0))
    u_next = jnp.where(rows == tl - 1, next_row, pltpu.roll(u, tl - 1, 0))
    uc = u_prev * w_ref[0:1, :] + u * w_ref[1:2, :] + u_next * w_ref[2:3, :] + b_ref[...]
    c = x0_ref.shape[2]
    x0_ref[0] = uc[:, :c]
    z_ref[0] = uc[:, 2 * c:] * uc[:, c:2 * c]


def _hy_pre(hy3, conv_w, conv_b, tl=512, halo=8):
    b, seq, w3 = hy3.shape
    c = w3 // 3
    tl = min(tl, seq)
    per = tl // halo
    n_halo = seq // halo
    out = jax.ShapeDtypeStruct((b, seq, c), F32)
    return pl.pallas_call(
        _hy_pre_kernel,
        grid=(b, seq // tl),
        in_specs=[
            pl.BlockSpec((1, tl, w3), lambda bi, i: (bi, i, 0)),
            pl.BlockSpec((1, halo, w3), lambda bi, i: (bi, jnp.maximum(i * per - 1, 0), 0)),
            pl.BlockSpec((1, halo, w3), lambda bi, i: (bi, jnp.minimum((i + 1) * per, n_halo - 1), 0)),
            pl.BlockSpec((SHORT_CONV, w3), lambda bi, i: (0, 0)),
            pl.BlockSpec((1, w3), lambda bi, i: (0, 0)),
        ],
        out_specs=[pl.BlockSpec((1, tl, c), lambda bi, i: (bi, i, 0))] * 2,
        out_shape=[out, out],
        compiler_params=_params("parallel", "parallel"),
        name="hyena_pre",
    )(hy3, hy3, hy3, conv_w, conv_b.reshape(1, w3))


def _dft1_kernel(x_ref, f_ref, out_ref):
    parts = [x_ref[comp, 0] for comp in range(x_ref.shape[0])]
    out_ref[0] = jnp.dot(f_ref[...], _stack3_rows(parts), preferred_element_type=F32)


def _dft1(x4, f3, tn=2048):
    ncomp, p, k, n = x4.shape
    m = f3.shape[0]
    tn = min(tn, n)
    return pl.pallas_call(
        _dft1_kernel,
        grid=(p, n // tn),
        in_specs=[
            pl.BlockSpec((ncomp, 1, k, tn), lambda pi, j: (0, pi, 0, j)),
            pl.BlockSpec(f3.shape, lambda pi, j: (0, 0)),
        ],
        out_specs=pl.BlockSpec((1, m, tn), lambda pi, j: (pi, 0, j)),
        out_shape=jax.ShapeDtypeStruct((p, m, n), F32),
        compiler_params=_params("parallel", "parallel"),
        name="hyena_dft1",
    )(x4, f3)


def _filt2_kernel(a_ref, gh_ref, gl_ref, out_ref, *, scale):
    for j in range(a_ref.shape[0]):
        out_ref[j] = _dot3(gh_ref[j], gl_ref[j], a_ref[j]) * scale


def _filt2(a3, g_hi, g_lo, scale, kb):
    n1, m, c = a3.shape
    blk = lambda a: pl.BlockSpec((kb,) + a.shape[1:], lambda i: (i, 0, 0))
    return pl.pallas_call(
        functools.partial(_filt2_kernel, scale=scale),
        grid=(n1 // kb,),
        in_specs=[blk(a3), blk(g_hi), blk(g_lo)],
        out_specs=blk(a3),
        out_shape=jax.ShapeDtypeStruct(a3.shape, F32),
        compiler_params=_params("parallel"),
        name="hyena_filter_fft",
    )(a3, g_hi, g_lo)


def _mid_kernel(a_ref, kf_ref, gh_ref, gl_ref, gth_ref, gtl_ref, out_ref):
    n2 = a_ref.shape[2] // 2
    for j in range(a_ref.shape[1]):
        x = _dot3(gh_ref[j], gl_ref[j], a_ref[0, j])
        kf = kf_ref[j]
        xr, xi = x[:n2], x[n2:]
        kr, ki = kf[:n2], kf[n2:]
        y = jnp.concatenate([xr * kr - xi * ki, xr * ki + xi * kr], axis=0)
        out_ref[0, j] = _dot3(gth_ref[j], gtl_ref[j], y)


def _mid(a4, kf3, g_hi, g_lo, gt_hi, gt_lo, kb):
    p, n1, m, c = a4.shape
    data = pl.BlockSpec((1, kb, m, c), lambda i, pi: (pi, i, 0, 0))
    tab = lambda a: pl.BlockSpec((kb,) + a.shape[1:], lambda i, pi: (i, 0, 0))
    return pl.pallas_call(
        _mid_kernel,
        grid=(n1 // kb, p),
        in_specs=[data, tab(kf3), tab(g_hi), tab(g_lo), tab(gt_hi), tab(gt_lo)],
        out_specs=data,
        out_shape=jax.ShapeDtypeStruct(a4.shape, F32),
        compiler_params=_params("parallel", "arbitrary"),
        name="hyena_mid",
    )(a4, kf3, g_hi, g_lo, gt_hi, gt_lo)


def _dft3_kernel(u_ref, f_ref, z_ref, x0_ref, bias_ref, out_ref):
    u_hi, u_lo = _split_hl(u_ref[0])
    y = jnp.dot(f_ref[...], jnp.concatenate([u_hi, u_lo, u_hi], axis=0), preferred_element_type=F32)
    h1 = out_ref.shape[2]
    for comp in range(2):
        out_ref[comp, 0] = (y[comp * h1:(comp + 1) * h1] + z_ref[comp, 0] * bias_ref[...]) * x0_ref[comp, 0]


def _dft3(u3, f3, z4, x04, bias_t, tn=2048):
    p, m, n = u3.shape
    h1 = z4.shape[2]
    tn = min(tn, n)
    pair = pl.BlockSpec((2, 1, h1, tn), lambda pi, j: (0, pi, 0, j))
    return pl.pallas_call(
        _dft3_kernel,
        grid=(p, n // tn),
        in_specs=[
            pl.BlockSpec((1, m, tn), lambda pi, j: (pi, 0, j)),
            pl.BlockSpec(f3.shape, lambda pi, j: (0, 0)),
            pair, pair,
            pl.BlockSpec((1, tn), lambda pi, j: (0, j)),
        ],
        out_specs=pair,
        out_shape=jax.ShapeDtypeStruct(z4.shape, F32),
        compiler_params=_params("parallel", "parallel"),
        name="hyena_dft3",
    )(u3, f3, z4, x04, bias_t)


def _dft_tables(seq):
    n = 2 * seq
    lg = n.bit_length() - 1
    assert n == 1 << lg
    n2 = 1 << (lg // 2)
    n1 = n // n2
    h1 = n1 // 2
    i1 = jnp.arange(n1, dtype=jnp.int32)
    i2 = jnp.arange(n2, dtype=jnp.int32)
    ang1 = (2.0 * math.pi / n1) * ((i1[:, None] * i1[None, :]) % n1).astype(F32)
    wr, wi = jnp.cos(ang1), -jnp.sin(ang1)

    def rows(a, b):
        return jnp.stack([a, b], axis=1).reshape(2 * a.shape[0], a.shape[1])

    def cols(a, b):
        return jnp.stack([a, b], axis=2).reshape(a.shape[0], 2 * a.shape[1])

    f_data = jnp.concatenate([rows(wr[:, :h1], wi[:, :h1]), rows(-wi[:, :h1], wr[:, :h1])], axis=1)
    f_filt = rows(wr, wi)
    vr, vi = wr.T[:h1], -wi.T[:h1]
    f_inv = jnp.concatenate([cols(vr, -vi), cols(vi, vr)], axis=0)
    kk = i1[:, None, None] + n1 * i2[None, :, None]
    ang2 = (2.0 * math.pi / n) * ((kk * i2[None, None, :]) % n).astype(F32)
    gr, gi = jnp.cos(ang2), -jnp.sin(ang2)
    g = jnp.concatenate([jnp.concatenate([gr, -gi], axis=2), jnp.concatenate([gi, gr], axis=2)], axis=1)
    gt = jnp.swapaxes(g, 1, 2)

    def stack3(f):
        hi, lo = _split_hl(f)
        return jnp.concatenate([hi, hi, lo], axis=1)

    g_hi, g_lo = _split_hl(g)
    gt_hi, gt_lo = _split_hl(gt)
    return dict(n1=n1, n2=n2, h1=h1, f_data=stack3(f_data), f_filt=stack3(f_filt), f_inv=stack3(f_inv),
                g_hi=g_hi, g_lo=g_lo, gt_hi=gt_hi, gt_lo=gt_lo)


def _hyena(hy3, conv_w, conv_b, kern, filt_bias, tabs, rows_per_step=1024):
    b, seq, _ = hy3.shape
    c = D_HYENA
    assert b % 2 == 0
    p = b // 2
    n1, n2, h1 = tabs["n1"], tabs["n2"], tabs["h1"]
    lanes = n2 * c
    kb = max(1, min(n1, rows_per_step // (2 * n2)))
    z, x0 = _hy_pre(hy3, conv_w, conv_b)
    z4 = z.reshape(2, p, h1, lanes)
    x04 = x0.reshape(2, p, h1, lanes)
    a = _dft1(z4, tabs["f_data"])
    fa = _dft1(kern.reshape(1, 1, n1, lanes), tabs["f_filt"])
    kf = _filt2(fa.reshape(n1, 2 * n2, c), tabs["g_hi"], tabs["g_lo"], 1.0 / (2 * seq), kb)
    u = _mid(a.reshape(p, n1, 2 * n2, c), kf, tabs["g_hi"], tabs["g_lo"], tabs["gt_hi"], tabs["gt_lo"], kb)
    bias_t = jnp.tile(filt_bias, n2).reshape(1, lanes)
    y = _dft3(u.reshape(p, 2 * n1, lanes), tabs["f_inv"], z4, x04, bias_t)
    return y.reshape(b, seq, c)


def _rope_tables(seq):
    rows_n = seq // GRID_W
    row = jnp.repeat(jnp.arange(rows_n, dtype=F32), GRID_W)
    col = jnp.tile(jnp.arange(GRID_W, dtype=F32), rows_n)
    half = HEAD_DIM // 2
    inv = ROPE_THETA ** (-jnp.arange(0, half, 2, dtype=F32) / half)
    ang_r = row[:, None] * inv[None, :]
    ang_c = col[:, None] * inv[None, :]
    ang = jnp.concatenate([ang_r, ang_r, ang_c, ang_c], axis=-1)
    cos, sin = jnp.cos(ang), jnp.sin(ang)
    lower = (jnp.arange(HEAD_DIM) % half) < (half // 2)
    sa = jnp.where(lower[None, :], -sin, 0.0)
    sb = jnp.where(lower[None, :], 0.0, sin)
    return cos, sa, sb


def _hyena_filters(seq, w1, b1, f1, w2, b2, f2, w3):
    pos = jnp.arange(seq, dtype=F32)
    t = jnp.linspace(0.0, 1.0, seq, dtype=F32)
    bands = jnp.arange(1, POS_BANDS + 1, dtype=F32)
    ang = (2.0 * math.pi / seq) * pos[:, None] * bands[None, :]
    z = jnp.concatenate([t[:, None], jnp.cos(ang), jnp.sin(ang)], axis=-1)
    hp = lax.Precision.HIGHEST
    h = jnp.sin(f1 * (jnp.dot(z, w1, precision=hp) + b1))
    h = jnp.sin(f2 * (jnp.dot(h, w2, precision=hp) + b2))
    h = jnp.dot(h, w3, precision=hp)
    max_decay = math.log(DECAY_TARGET) / DECAY_FAST_PCT
    min_decay = math.log(DECAY_TARGET) / DECAY_SLOW_PCT
    deltas = jnp.linspace(min_decay, max_decay, D_HYENA, dtype=F32)
    decay = jnp.exp(-t[:, None] * jnp.abs(deltas)[None, :])
    hf = h[:, :D_HYENA] * decay
    hb = h[:, D_HYENA:] * decay
    kern = jnp.concatenate([hf[:1] + hb[:1], hf[1:], jnp.zeros((1, D_HYENA), F32), hb[:0:-1]], axis=0)
    return kern / jnp.sum(jnp.abs(kern), axis=0, keepdims=True)


def _run_trunk(x, mem, wts):
    b, seq, d = x.shape
    n = b * seq
    n_mem = mem.shape[1]
    depth = wts["w_in"].shape[0]
    cos, sa, sb = _rope_tables(seq)
    tabs = _dft_tables(seq)
    cap = (EC_CAPACITY * n) // N_EXPERTS
    mem2 = mem.reshape(b * n_mem, d)
    x2 = x.reshape(n, d)
    row = lambda a: a.reshape(1, -1)
    for l in range(depth):
        qkv, hy, gates = _in_proj(x2, row(wts["g_mix"][l]), wts["w_in"][l], cos, sa, sb,
                                  row(wts["q_norm"][l]), row(wts["k_norm"][l]), seq)
        o = _attention(qkv.reshape(b, seq, -1))
        kern = _hyena_filters(seq, wts["filt_w1"][l], wts["filt_b1"][l], wts["filt_freq1"][l],
                              wts["filt_w2"][l], wts["filt_b2"][l], wts["filt_freq2"][l], wts["filt_w3"][l])
        y = _hyena(hy.reshape(b, seq, -1), wts["conv_w"][l], wts["conv_b"][l], kern, wts["filt_bias"][l], tabs)
        x2 = _mix(o.reshape(n, -1), y.reshape(n, -1), gates, x2,
                  wts["w_attn_o"][l], wts["w_hyena_o"][l], wts["w_out"][l])
        kv = _mem_kv(mem2, row(wts["g_mem"][l]), wts["w_ckv"][l], row(wts["ck_norm"][l]))
        x3 = _cross(x2.reshape(b, seq, d), row(wts["g_cross"][l]), wts["w_cq"][l], row(wts["cq_norm"][l]),
                    kv.reshape(b, n_mem, 2 * d), wts["w_co"][l])
        x2 = x3.reshape(n, d)
        hb, aff_t = _ffn_pre(x2, row(wts["g_ffn"][l]), wts["w_router_t"][l])
        gsel, idx = lax.top_k(aff_t, cap)
        xe = hb[idx]
        ye = _experts(xe, wts["w_gate"][l], wts["w_up"][l], wts["w_down"][l], gsel[..., None])
        x2 = x2.at[idx.reshape(-1)].add(ye.reshape(-1, d))
    return x2.reshape(b, seq, d)


def kernel(x_prompt, x_sample, mem_prompt, mem_sample, g_mix, w_in, q_norm, k_norm, conv_w, conv_b,
           filt_w1, filt_b1, filt_freq1, filt_w2, filt_b2, filt_freq2, filt_w3, filt_bias, w_attn_o,
           w_hyena_o, w_out, g_cross, g_mem, w_cq, w_ckv, cq_norm, ck_norm, w_co, g_ffn, w_router,
           w_gate, w_up, w_down):
    wts = dict(
        g_mix=g_mix, w_in=w_in.astype(BF16), q_norm=q_norm, k_norm=k_norm, conv_w=conv_w, conv_b=conv_b,
        filt_w1=filt_w1, filt_b1=filt_b1, filt_freq1=filt_freq1, filt_w2=filt_w2, filt_b2=filt_b2,
        filt_freq2=filt_freq2, filt_w3=filt_w3, filt_bias=filt_bias,
        w_attn_o=w_attn_o.astype(BF16), w_hyena_o=w_hyena_o.astype(BF16), w_out=w_out.astype(BF16),
        g_cross=g_cross, g_mem=g_mem, w_cq=w_cq.astype(BF16), w_ckv=w_ckv.astype(BF16),
        cq_norm=cq_norm, ck_norm=ck_norm, w_co=w_co.astype(BF16), g_ffn=g_ffn,
        w_router_t=jnp.swapaxes(w_router, 1, 2), w_gate=w_gate, w_up=w_up, w_down=w_down,
    )
    y_prompt = _run_trunk(x_prompt, mem_prompt, wts)
    y_sample = _run_trunk(x_sample, mem_sample, wts)
    return (y_prompt, y_sample)
```
